```python
import math
import jax, jax.numpy as jnp
from jax import lax
import numpy as np

D_MODEL = 1024
BATCH = 4
SEQ = 4096
DEPTH = 4
DEC_BATCH = 128
DEC_SEQ = 4
PAST_LEN = 8192
PAGE_SIZE = 128

N_MIXERS = 4
BLOCK = 128
SB_HEADS = 16
SB_HEAD_DIM = D_MODEL // SB_HEADS
SB_BIAS_INIT = -6.0
GC_WIDTH = 3
SWA_HEADS = 16
SWA_KV_HEADS = 4
SWA_HEAD_DIM = 64
SWA_GROUP = SWA_HEADS // SWA_KV_HEADS
WINDOW = 128
REL_BUCKETS = 32
REL_MAX_DIST = 128
CF_WIDTH = 31
D_FF = -(-8 * D_MODEL // (3 * 256)) * 256
DEEP_ALPHA = (2 * DEPTH) ** 0.25
DEEP_BETA = (8 * DEPTH) ** -0.25
LN_EPS = 1e-5

kernel_name = "hybrid_sb_gconv_swa_conformer_decoder_step"


def layer_norm(x, g, b):
    xf = x.astype(jnp.float32)
    mu = xf.mean(-1, keepdims=True)
    var = jnp.square(xf - mu).mean(-1, keepdims=True)
    return ((xf - mu) * lax.rsqrt(var + LN_EPS) * g + b).astype(x.dtype)


def swiglu(x, w_gate, w_up, w_down):
    return (jax.nn.silu(x @ w_gate) * (x @ w_up)) @ w_down


def causal_depthwise_conv(u, prefix, w, b):
    full = jnp.concatenate([prefix, u], axis=1)
    y = lax.conv_general_dilated(full, w[:, None, :].astype(full.dtype), window_strides=(1,), padding='VALID',
                                 dimension_numbers=('NWC', 'WIO', 'NWC'), feature_group_count=u.shape[-1])
    return y + b, full[:, -(w.shape[0] - 1):]


def stick_breaking_weights(z, mask):
    z = z.astype(jnp.float32)
    log_1m = jnp.where(mask, jax.nn.log_sigmoid(-z), 0.0)
    after = lax.cumsum(log_1m, axis=z.ndim - 1, reverse=True) - log_1m
    return jnp.where(mask, jnp.exp(jax.nn.log_sigmoid(z) + after), 0.0)


def sb_split(x, w_qkv):
    q, k, v = jnp.split(x @ w_qkv, 3, axis=-1)
    shape = x.shape[:-1] + (SB_HEADS, SB_HEAD_DIM)
    return q.reshape(shape), k.reshape(shape), v.reshape(shape)


def sb_prompt(x, w_qkv, sb_bias, w_o):
    B, S, _ = x.shape
    nb = S // BLOCK
    q, k, v = sb_split(x, w_qkv)
    qb = q.reshape(B, nb, BLOCK, SB_HEADS, SB_HEAD_DIM).transpose(1, 0, 2, 3, 4)
    kpos = jnp.arange(S)
    scale = SB_HEAD_DIM ** -0.5
    bias = sb_bias.astype(jnp.float32)[None, :, None, None]

    def block(args):
        q_blk, n = args
        qpos = n * BLOCK + jnp.arange(BLOCK)
        z = jnp.einsum('bqhd,bkhd->bhqk', q_blk, k).astype(jnp.float32) * scale + bias
        a = stick_breaking_weights(z, kpos[None, :] < qpos[:, None])
        return jnp.einsum('bhqk,bkhd->bqhd', a.astype(v.dtype), v)

    o = lax.map(block, (qb, jnp.arange(nb)))
    o = o.transpose(1, 0, 2, 3, 4).reshape(B, S, SB_HEADS * SB_HEAD_DIM)
    return o @ w_o, k, v


def sb_sample(x, cache_k, cache_v, page_table, w_qkv, sb_bias, w_o):
    N, T, _ = x.shape
    past = page_table.shape[1] * PAGE_SIZE
    q, k, v = sb_split(x, w_qkv)
    qpos = past + jnp.arange(T)
    kpos = jnp.arange(past + T)
    mask = kpos[None, :] < qpos[:, None]
    scale = SB_HEAD_DIM ** -0.5
    bias = sb_bias.astype(jnp.float32)[:, None, None]

    def one_seq(args):
        q_s, k_s, v_s, pages = args
        k_all = jnp.concatenate([cache_k[pages].reshape(past, SB_HEADS, SB_HEAD_DIM), k_s], axis=0)
        v_all = jnp.concatenate([cache_v[pages].reshape(past, SB_HEADS, SB_HEAD_DIM), v_s], axis=0)
        z = jnp.einsum('qhd,khd->hqk', q_s, k_all).astype(jnp.float32) * scale + bias
        a = stick_breaking_weights(z, mask)
        return jnp.einsum('hqk,khd->qhd', a.astype(v_all.dtype), v_all)

    o = lax.map(one_seq, (q, k, v, page_table))
    return o.reshape(N, T, SB_HEADS * SB_HEAD_DIM) @ w_o, k, v


def gc_mix(x, prefix, w_in, conv_w, conv_b, w_out):
    b_gate, c_gate, h = jnp.split(x @ w_in, 3, axis=-1)
    y, new_state = causal_depthwise_conv(c_gate * h, prefix, conv_w, conv_b)
    return (b_gate * y) @ w_out, new_state


def t5_bucket(dist):
    d = jnp.maximum(dist, 0)
    max_exact = REL_BUCKETS // 2
    large = max_exact + (jnp.log(jnp.maximum(d, 1).astype(jnp.float32) / max_exact)
                         / math.log(REL_MAX_DIST / max_exact) * (REL_BUCKETS - max_exact)).astype(jnp.int32)
    large = jnp.minimum(large, REL_BUCKETS - 1)
    return jnp.where(d < max_exact, d, large)


def rel_bias_heads(dist, rel_bias):
    bias = jnp.moveaxis(rel_bias[t5_bucket(dist)].astype(jnp.float32), -1, 0)
    return bias.reshape(SWA_KV_HEADS, SWA_GROUP, *dist.shape)


def sink_softmax(logits, mask, sinks):
    logits = jnp.where(mask, logits, -jnp.inf)
    m = jnp.maximum(logits.max(-1, keepdims=True), sinks)
    p = jnp.exp(logits - m)
    return p / (p.sum(-1, keepdims=True) + jnp.exp(sinks - m))


def swa_split(x, w_qkv):
    qd = SWA_HEADS * SWA_HEAD_DIM
    kd = SWA_KV_HEADS * SWA_HEAD_DIM
    h = x @ w_qkv
    lead = x.shape[:-1]
    q = h[..., :qd].reshape(lead + (SWA_KV_HEADS, SWA_GROUP, SWA_HEAD_DIM))
    k = h[..., qd:qd + kd].reshape(lead + (SWA_KV_HEADS, SWA_HEAD_DIM))
    v = h[..., qd + kd:].reshape(lead + (SWA_KV_HEADS, SWA_HEAD_DIM))
    return q, k, v


def swa_prompt(x, w_qkv, sinks, w_o, rel_bias):
    B, S, _ = x.shape
    nb = S // BLOCK
    q, k, v = swa_split(x, w_qkv)
    pad = jnp.zeros((B, BLOCK, SWA_KV_HEADS, SWA_HEAD_DIM), k.dtype)

    def band(t):
        tb = jnp.concatenate([pad, t], axis=1).reshape(B, nb + 1, BLOCK, SWA_KV_HEADS, SWA_HEAD_DIM)
        return jnp.concatenate([tb[:, :-1], tb[:, 1:]], axis=2)

    kb, vb = band(k), band(v)
    qb = q.reshape(B, nb, BLOCK, SWA_KV_HEADS, SWA_GROUP, SWA_HEAD_DIM)
    dist = jnp.arange(BLOCK)[:, None] - jnp.arange(2 * BLOCK)[None, :] + BLOCK
    kpos = jnp.arange(nb)[:, None] * BLOCK - BLOCK + jnp.arange(2 * BLOCK)[None, :]
    mask = ((dist >= 0) & (dist <= WINDOW))[None] & (kpos >= 0)[:, None, :]
    logits = (jnp.einsum('bnqkgd,bnskd->bnkgqs', qb, kb).astype(jnp.float32) * SWA_HEAD_DIM ** -0.5
              + rel_bias_heads(dist, rel_bias))
    sk = sinks.astype(jnp.float32).reshape(SWA_KV_HEADS, SWA_GROUP, 1, 1)
    p = sink_softmax(logits, mask[None, :, None, None], sk)
    o = jnp.einsum('bnkgqs,bnskd->bnqkgd', p.astype(vb.dtype), vb).reshape(B, S, SWA_HEADS * SWA_HEAD_DIM)
    buf = min(WINDOW, S)
    return o @ w_o, k[:, -buf:], v[:, -buf:]


def swa_sample(x, cache_k, cache_v, w_qkv, sinks, w_o, rel_bias):
    N, T, _ = x.shape
    W = cache_k.shape[1]
    q, k, v = swa_split(x, w_qkv)
    k_all = jnp.concatenate([cache_k, k], axis=1)
    v_all = jnp.concatenate([cache_v, v], axis=1)
    dist = jnp.arange(T)[:, None] + W - jnp.arange(W + T)[None, :]
    mask = (dist >= 0) & (dist <= WINDOW)
    logits = (jnp.einsum('btkgd,bskd->bkgts', q, k_all).astype(jnp.float32) * SWA_HEAD_DIM ** -0.5
              + rel_bias_heads(dist, rel_bias))
    sk = sinks.astype(jnp.float32).reshape(SWA_KV_HEADS, SWA_GROUP, 1, 1)
    p = sink_softmax(logits, mask, sk)
    o = jnp.einsum('bkgts,bskd->btkgd', p.astype(v_all.dtype), v_all).reshape(N, T, SWA_HEADS * SWA_HEAD_DIM)
    return o @ w_o, k_all[:, -W:], v_all[:, -W:]


def cf_mix(x, prefix, w_in, conv_w, conv_b, ln_g, ln_b, w_out):
    a, g = jnp.split(x @ w_in, 2, axis=-1)
    y, new_state = causal_depthwise_conv(a * jax.nn.sigmoid(g), prefix, conv_w, conv_b)
    y = jax.nn.silu(layer_norm(y, ln_g, ln_b))
    return y @ w_out, new_state


def setup_inputs(seed: int = 0) -> dict:
    key = jax.random.key(seed)
    ks = jax.random.split(key, 40)
    f32 = jnp.float32

    def nrm(k, shape, scale=1.0):
        return jax.random.normal(k, shape, f32) * scale

    n_pages = PAST_LEN // PAGE_SIZE
    n_used = DEC_BATCH * n_pages
    n_pool = n_used + n_used // 4
    win_buf = min(WINDOW, PAST_LEN)
    D = D_MODEL
    qkv_swa = SWA_HEADS * SWA_HEAD_DIM + 2 * SWA_KV_HEADS * SWA_HEAD_DIM
    return {
        "x_prompt": nrm(ks[0], (BATCH, SEQ, D)),
        "x_sample": nrm(ks[1], (DEC_BATCH, DEC_SEQ, D)),
        "cache_sb_k": nrm(ks[2], (n_pool, PAGE_SIZE, SB_HEADS, SB_HEAD_DIM)),
        "cache_sb_v": nrm(ks[3], (n_pool, PAGE_SIZE, SB_HEADS, SB_HEAD_DIM)),
        "page_table": jax.random.permutation(ks[4], n_pool)[:n_used].reshape(DEC_BATCH, n_pages).astype(jnp.int32),
        "state_gc_conv": nrm(ks[5], (DEC_BATCH, GC_WIDTH - 1, D)),
        "cache_swa_k": nrm(ks[6], (DEC_BATCH, win_buf, SWA_KV_HEADS, SWA_HEAD_DIM)),
        "cache_swa_v": nrm(ks[7], (DEC_BATCH, win_buf, SWA_KV_HEADS, SWA_HEAD_DIM)),
        "state_cf_conv": nrm(ks[8], (DEC_BATCH, CF_WIDTH - 1, D)),
        "sb_w_qkv": nrm(ks[9], (D, 3 * SB_HEADS * SB_HEAD_DIM), D ** -0.5),
        "sb_bias": SB_BIAS_INIT + nrm(ks[30], (SB_HEADS,), 0.5),
        "sb_w_o": nrm(ks[10], (SB_HEADS * SB_HEAD_DIM, D), DEEP_BETA * D ** -0.5),
        "gc_w_in": nrm(ks[11], (D, 3 * D), D ** -0.5),
        "gc_conv_w": nrm(ks[12], (GC_WIDTH, D), GC_WIDTH ** -0.5),
        "gc_conv_b": nrm(ks[13], (D,), 0.01),
        "gc_w_out": nrm(ks[14], (D, D), DEEP_BETA * D ** -0.5),
        "swa_w_qkv": nrm(ks[15], (D, qkv_swa), D ** -0.5),
        "swa_sinks": nrm(ks[16], (SWA_HEADS,), 0.5),
        "swa_w_o": nrm(ks[17], (SWA_HEADS * SWA_HEAD_DIM, D), DEEP_BETA * (SWA_HEADS * SWA_HEAD_DIM) ** -0.5),
        "rel_bias": nrm(ks[18], (REL_BUCKETS, SWA_HEADS), 0.5),
        "cf_w_in": nrm(ks[19], (D, 2 * D), D ** -0.5),
        "cf_conv_w": nrm(ks[20], (CF_WIDTH, D), CF_WIDTH ** -0.5),
        "cf_conv_b": nrm(ks[21], (D,), 0.01),
        "cf_ln_g": 1.0 + nrm(ks[22], (D,), 0.1),
        "cf_ln_b": nrm(ks[23], (D,), 0.01),
        "cf_w_out": nrm(ks[24], (D, D), DEEP_BETA * D ** -0.5),
        "ffn_w_gate": nrm(ks[25], (DEPTH, D, D_FF), D ** -0.5),
        "ffn_w_up": nrm(ks[26], (DEPTH, D, D_FF), D ** -0.5),
        "ffn_w_down": nrm(ks[27], (DEPTH, D_FF, D), DEEP_BETA * D_FF ** -0.5),
        "ln_g": 1.0 + nrm(ks[28], (DEPTH, 2, D), 0.1),
        "ln_b": nrm(ks[29], (DEPTH, 2, D), 0.01),
    }


def reference(x_prompt, x_sample, cache_sb_k, cache_sb_v, page_table, state_gc_conv, cache_swa_k, cache_swa_v,
              state_cf_conv, sb_w_qkv, sb_bias, sb_w_o, gc_w_in, gc_conv_w, gc_conv_b, gc_w_out, swa_w_qkv, swa_sinks,
              swa_w_o, rel_bias, cf_w_in, cf_conv_w, cf_conv_b, cf_ln_g, cf_ln_b, cf_w_out,
              ffn_w_gate, ffn_w_up, ffn_w_down, ln_g, ln_b):
    xp, xs = x_prompt, x_sample
    B = xp.shape[0]
    for i in range(DEPTH):
        m = i % N_MIXERS
        if m == 0:
            dp, sb_k_p, sb_v_p = sb_prompt(xp, sb_w_qkv, sb_bias, sb_w_o)
            ds, sb_k_s, sb_v_s = sb_sample(xs, cache_sb_k, cache_sb_v, page_table, sb_w_qkv, sb_bias, sb_w_o)
        elif m == 1:
            zp = jnp.zeros((B, GC_WIDTH - 1, D_MODEL), xp.dtype)
            dp, gc_p = gc_mix(xp, zp, gc_w_in, gc_conv_w, gc_conv_b, gc_w_out)
            ds, gc_s = gc_mix(xs, state_gc_conv, gc_w_in, gc_conv_w, gc_conv_b, gc_w_out)
        elif m == 2:
            dp, swa_k_p, swa_v_p = swa_prompt(xp, swa_w_qkv, swa_sinks, swa_w_o, rel_bias)
            ds, swa_k_s, swa_v_s = swa_sample(xs, cache_swa_k, cache_swa_v, swa_w_qkv, swa_sinks, swa_w_o, rel_bias)
        else:
            zp = jnp.zeros((B, CF_WIDTH - 1, D_MODEL), xp.dtype)
            dp, cf_p = cf_mix(xp, zp, cf_w_in, cf_conv_w, cf_conv_b, cf_ln_g, cf_ln_b, cf_w_out)
            ds, cf_s = cf_mix(xs, state_cf_conv, cf_w_in, cf_conv_w, cf_conv_b, cf_ln_g, cf_ln_b, cf_w_out)
        xp = layer_norm(DEEP_ALPHA * xp + dp, ln_g[i, 0], ln_b[i, 0])
        xs = layer_norm(DEEP_ALPHA * xs + ds, ln_g[i, 0], ln_b[i, 0])
        xp = layer_norm(DEEP_ALPHA * xp + swiglu(xp, ffn_w_gate[i], ffn_w_up[i], ffn_w_down[i]), ln_g[i, 1], ln_b[i, 1])
        xs = layer_norm(DEEP_ALPHA * xs + swiglu(xs, ffn_w_gate[i], ffn_w_up[i], ffn_w_down[i]), ln_g[i, 1], ln_b[i, 1])
    return (xp, xs, sb_k_p, sb_v_p, sb_k_s, sb_v_s, gc_p, gc_s, swa_k_p, swa_v_p, swa_k_s, swa_v_s, cf_p, cf_s)
```

```python
import functools
import math

import numpy as np
import jax
import jax.numpy as jnp
from jax import lax
from jax.experimental import pallas as pl
from jax.experimental.pallas import tpu as pltpu

F32 = jnp.float32
BF16 = jnp.bfloat16

DEPTH = 4
PAGE_SIZE = 128
SB_HEADS = 16
SB_HEAD_DIM = 64
GC_WIDTH = 3
SWA_HEADS = 16
SWA_KV_HEADS = 4
SWA_HEAD_DIM = 64
SWA_GROUP = SWA_HEADS // SWA_KV_HEADS
WINDOW = 128
REL_BUCKETS = 32
REL_MAX_DIST = 128
CF_WIDTH = 31
DEEP_ALPHA = (2 * DEPTH) ** 0.25
LN_EPS = 1e-5

LANES = 128
SUBLANES = 8
VMEM_LIMIT = 56 * 1024 * 1024
TM = 512
TN = 512
FFN_STEPS = 2
KEY_BLOCK = 128
SB_ROWS = 512
SB_PAGES_PER_STEP = 4
CONV_ROWS = 16
CONV_TILE = 128
CONV_SEQS = 16

NT_DIMS = (((1,), (1,)), ((), ()))


def _params(*sem):
    return pltpu.CompilerParams(dimension_semantics=sem, vmem_limit_bytes=VMEM_LIMIT)


def _dot(a, b):
    return jnp.dot(a, b, preferred_element_type=F32)


def _dot_nt(a, b):
    return lax.dot_general(a, b, NT_DIMS, preferred_element_type=F32)


def _layer_norm(x, g, b):
    mu = jnp.mean(x, axis=-1, keepdims=True)
    xc = x - mu
    var = jnp.mean(xc * xc, axis=-1, keepdims=True)
    return xc * lax.rsqrt(var + LN_EPS) * g + b


def _mm_kernel(x_ref, *refs, n_w, n_wt, epilogue):
    w_refs, wt_refs, out_refs = refs[:n_w], refs[n_w:n_w + n_wt], refs[n_w + n_wt:]
    xb = x_ref[0].astype(BF16)
    parts = [_dot(xb, w[...]) for w in w_refs]
    tparts = [_dot_nt(wt[...], xb) for wt in wt_refs]
    for o_ref, val in zip(out_refs, epilogue(parts, tparts)):
        o_ref[0] = val.astype(o_ref.dtype)


def _mm(x, ws, wts, epilogue, outs, tn=TN):
    g, s, k = x.shape
    n = ws[0].shape[1] if ws else wts[0].shape[0]
    tm = min(TM, s)
    out_specs, out_shape = [], []
    for kind, dt in outs:
        if kind == 'n':
            out_specs.append(pl.BlockSpec((1, tm, tn), lambda gi, i, j: (gi, i, j)))
            out_shape.append(jax.ShapeDtypeStruct((g, s, n), dt))
        else:
            out_specs.append(pl.BlockSpec((1, tn, tm), lambda gi, i, j: (gi, j, i)))
            out_shape.append(jax.ShapeDtypeStruct((g, n, s), dt))
    return pl.pallas_call(
        functools.partial(_mm_kernel, n_w=len(ws), n_wt=len(wts), epilogue=epilogue),
        grid=(g, s // tm, n // tn),
        in_specs=[pl.BlockSpec((1, tm, k), lambda gi, i, j: (gi, i, 0))]
        + [pl.BlockSpec((k, tn), lambda gi, i, j: (0, j)) for _ in ws]
        + [pl.BlockSpec((tn, k), lambda gi, i, j: (j, 0)) for _ in wts],
        out_specs=out_specs,
        out_shape=out_shape,
        compiler_params=_params("arbitrary", "arbitrary", "arbitrary"),
    )(x, *ws, *wts)


def _proj_ln_kernel(a_ref, w_ref, x_ref, g_ref, b_ref, o_ref):
    d = _dot(a_ref[...].astype(BF16), w_ref[...])
    o_ref[...] = _layer_norm(DEEP_ALPHA * x_ref[...] + d, g_ref[...], b_ref[...])


def _proj_ln(a, w, x, g, b):
    m, k = a.shape
    d = x.shape[1]
    tm = min(TM, m)
    return pl.pallas_call(
        _proj_ln_kernel,
        grid=(m // tm,),
        in_specs=[pl.BlockSpec((tm, k), lambda i: (i, 0)),
                  pl.BlockSpec((k, d), lambda i: (0, 0)),
                  pl.BlockSpec((tm, d), lambda i: (i, 0)),
                  pl.BlockSpec((1, d), lambda i: (0, 0)),
                  pl.BlockSpec((1, d), lambda i: (0, 0))],
        out_specs=pl.BlockSpec((tm, d), lambda i: (i, 0)),
        out_shape=jax.ShapeDtypeStruct((m, d), F32),
        compiler_params=_params("arbitrary"),
    )(a, w, x, g, b)


def _ffn_kernel(x_ref, wg_ref, wu_ref, wd_ref, g_ref, b_ref, o_ref, acc_ref, xb_ref):
    f = pl.program_id(1)

    @pl.when(f == 0)
    def _():
        xb_ref[...] = x_ref[...].astype(BF16)
        acc_ref[...] = jnp.zeros_like(acc_ref)

    xb = xb_ref[...]
    hg = _dot(xb, wg_ref[...])
    hu = _dot(xb, wu_ref[...])
    h = (hg * jax.nn.sigmoid(hg)) * hu
    acc_ref[...] += _dot(h.astype(BF16), wd_ref[...])

    @pl.when(f == pl.num_programs(1) - 1)
    def _():
        o_ref[...] = _layer_norm(DEEP_ALPHA * x_ref[...] + acc_ref[...], g_ref[...], b_ref[...])


def _ffn_ln(x, wg, wu, wd, g, b):
    m, d = x.shape
    d_ff = wg.shape[1]
    tf = d_ff // FFN_STEPS
    tm = min(TM, m)
    return pl.pallas_call(
        _ffn_kernel,
        grid=(m // tm, FFN_STEPS),
        in_specs=[pl.BlockSpec((tm, d), lambda i, f: (i, 0)),
                  pl.BlockSpec((d, tf), lambda i, f: (0, f)),
                  pl.BlockSpec((d, tf), lambda i, f: (0, f)),
                  pl.BlockSpec((tf, d), lambda i, f: (f, 0)),
                  pl.BlockSpec((1, d), lambda i, f: (0, 0)),
                  pl.BlockSpec((1, d), lambda i, f: (0, 0))],
        out_specs=pl.BlockSpec((tm, d), lambda i, f: (i, 0)),
        out_shape=jax.ShapeDtypeStruct((m, d), F32),
        scratch_shapes=[pltpu.VMEM((tm, d), F32), pltpu.VMEM((tm, d), BF16)],
        compiler_params=_params("arbitrary", "arbitrary"),
    )(x, wg, wu, wd, g, b)


def _suffix_matrix():
    j = np.arange(KEY_BLOCK)[:, None]
    s = np.arange(KEY_BLOCK)[None, :]
    strict = (j > s).astype(np.float32)
    return jnp.asarray(np.concatenate([strict, np.ones_like(strict)], axis=1), dtype=BF16)


def _sb_weights(z, mask, carry, u2):
    t = jnp.log1p(jnp.exp(-jnp.abs(z)))
    log_sig = jnp.minimum(z, 0.0) - t
    log_1m = jnp.minimum(-z, 0.0) - t
    if mask is not None:
        log_1m = jnp.where(mask, log_1m, 0.0)
    hi = log_1m.astype(BF16)
    lo = (log_1m - hi.astype(F32)).astype(BF16)
    sums = _dot(hi, u2) + _dot(lo, u2)
    after = sums[:, :KEY_BLOCK] + carry
    a = jnp.exp(log_sig + after)
    if mask is not None:
        a = jnp.where(mask, a, 0.0)
    return a, sums[:, KEY_BLOCK:]


def _sb_prompt_kernel(bias_ref, q_ref, kt_ref, vt_ref, u2_ref, o_ref, carry_ref, *, seq, scale):
    p = pl.program_id(1)
    j = pl.program_id(2)
    kb = pl.num_programs(2) - 1 - j

    @pl.when(j == 0)
    def _():
        o_ref[...] = jnp.zeros_like(o_ref)
        carry_ref[...] = jnp.zeros_like(carry_ref)

    kt = kt_ref[0]
    vt = vt_ref[0]
    u2 = u2_ref[...]
    lane = lax.broadcasted_iota(jnp.int32, (SB_ROWS, LANES), 1)
    rowi = lax.broadcasted_iota(jnp.int32, (SB_ROWS, LANES), 0)
    kpos = kb * KEY_BLOCK + lane
    low_half = lane < SB_HEAD_DIM

    def chunk(c, carry_unused):
        r0 = pl.multiple_of(c * SB_ROWS, SB_ROWS)
        qc = q_ref[0, pl.ds(r0, SB_ROWS), :]
        mask = kpos < (r0 + rowi)
        outs = []
        for hh in range(2):
            qh = jnp.where(low_half if hh == 0 else jnp.logical_not(low_half), qc, jnp.zeros_like(qc))
            z = _dot(qh, kt) * scale + bias_ref[2 * p + hh]
            a, tot = _sb_weights(z, mask, carry_ref[hh, pl.ds(r0, SB_ROWS), :], u2)
            carry_ref[hh, pl.ds(r0, SB_ROWS), :] += tot
            outs.append(_dot_nt(a.astype(BF16), vt))
        o_ref[0, pl.ds(r0, SB_ROWS), :] += jnp.where(low_half, outs[0], outs[1])
        return carry_unused

    lax.fori_loop((kb * KEY_BLOCK) // SB_ROWS, seq // SB_ROWS, chunk, 0)


def _sb_prompt_attn(q, kt, vt, sb_bias):
    b, s, d = q.shape
    nkb = s // KEY_BLOCK
    rows = min(SB_ROWS, s)
    kv_spec = pl.BlockSpec((1, LANES, KEY_BLOCK), lambda bi, p, j: (bi, p, nkb - 1 - j))
    return pl.pallas_call(
        functools.partial(_sb_prompt_kernel, seq=s, scale=SB_HEAD_DIM ** -0.5),
        grid=(b, d // LANES, nkb),
        in_specs=[pl.BlockSpec(memory_space=pltpu.SMEM),
                  pl.BlockSpec((1, s, LANES), lambda bi, p, j: (bi, 0, p)),
                  kv_spec, kv_spec,
                  pl.BlockSpec((KEY_BLOCK, 2 * KEY_BLOCK), lambda bi, p, j: (0, 0))],
        out_specs=pl.BlockSpec((1, s, LANES), lambda bi, p, j: (bi, 0, p)),
        out_shape=jax.ShapeDtypeStruct((b, s, d), F32),
        scratch_shapes=[pltpu.VMEM((2, s, LANES), F32)],
        compiler_params=_params("arbitrary", "arbitrary", "arbitrary"),
    )(sb_bias, q, kt, vt, _suffix_matrix())


def _sb_sample_kernel(pt_ref, q_ref, kn_ref, vn_ref, bias_ref, u2_ref, *rest, pp, t_new, scale):
    del pt_ref
    k_refs, v_refs = rest[:pp], rest[pp:2 * pp]
    o_ref = rest[2 * pp]
    qbd_ref, kpad_ref, vpad_ref, carry_ref, acc_ref = rest[2 * pp + 1:]
    j = pl.program_id(1)
    rows = t_new * SB_HEADS
    d = q_ref.shape[-1]
    u2 = u2_ref[...]
    bias = bias_ref[...]

    row_d = lax.broadcasted_iota(jnp.int32, (rows, d), 0)
    lane_d = lax.broadcasted_iota(jnp.int32, (rows, d), 1)
    own_head = (lane_d // SB_HEAD_DIM) == (row_d % SB_HEADS)

    def step(z, pv, mask):
        a, tot = _sb_weights(z * scale + bias, mask, carry_ref[...], u2)
        carry_ref[...] += tot
        acc_ref[...] += pv(a.astype(BF16))

    @pl.when(j == 0)
    def _():
        qv = q_ref[0]
        qrep = jnp.concatenate([jnp.broadcast_to(qv[t:t + 1, :], (SB_HEADS, d)) for t in range(t_new)], axis=0)
        qbd_ref[...] = jnp.where(own_head, qrep, 0.0).astype(BF16)
        kpad_ref[...] = jnp.zeros_like(kpad_ref)
        vpad_ref[...] = jnp.zeros_like(vpad_ref)
        kpad_ref[0:SUBLANES, :] = kn_ref[0]
        vpad_ref[0:SUBLANES, :] = vn_ref[0]
        carry_ref[...] = jnp.zeros_like(carry_ref)
        acc_ref[...] = jnp.zeros_like(acc_ref)
        row = lax.broadcasted_iota(jnp.int32, (rows, LANES), 0)
        lane = lax.broadcasted_iota(jnp.int32, (rows, LANES), 1)
        vnew = vpad_ref[...].astype(BF16)
        step(_dot_nt(qbd_ref[...], kpad_ref[...].astype(BF16)), lambda a: _dot(a, vnew), lane < row // SB_HEADS)

    for i in range(pp):
        kt = k_refs[i][0].reshape(d, PAGE_SIZE).astype(BF16)
        vt = v_refs[i][0].reshape(d, PAGE_SIZE).astype(BF16)
        step(_dot(qbd_ref[...], kt), lambda a, vt=vt: _dot_nt(a, vt), None)

    @pl.when(j == pl.num_programs(1) - 1)
    def _():
        acc = jnp.where(own_head, acc_ref[...], 0.0)
        o_ref[0] = jnp.concatenate(
            [jnp.sum(acc[t * SB_HEADS:(t + 1) * SB_HEADS, :], axis=0, keepdims=True) for t in range(t_new)], axis=0)


def _sb_sample_attn(q, k_new, v_new, cache_kt, cache_vt, page_table, sb_bias):
    n, t_new, d = q.shape
    n_pages = page_table.shape[1]
    pp = min(SB_PAGES_PER_STEP, n_pages)
    rows = t_new * SB_HEADS
    pad = ((0, 0), (0, SUBLANES - t_new), (0, 0))
    k8, v8 = jnp.pad(k_new, pad), jnp.pad(v_new, pad)
    bias_tile = jnp.broadcast_to(jnp.tile(sb_bias, t_new)[:, None], (rows, LANES))

    def page_spec(i):
        return pl.BlockSpec((1, SB_HEADS, SB_HEAD_DIM, PAGE_SIZE),
                            lambda s, j, pt: (pt[s, n_pages - 1 - (j * pp + i)], 0, 0, 0))

    per_seq = lambda s, j, pt: (s, 0, 0)
    const2 = lambda s, j, pt: (0, 0)
    grid_spec = pltpu.PrefetchScalarGridSpec(
        num_scalar_prefetch=1,
        grid=(n, n_pages // pp),
        in_specs=[pl.BlockSpec((1, t_new, d), per_seq),
                  pl.BlockSpec((1, SUBLANES, d), per_seq),
                  pl.BlockSpec((1, SUBLANES, d), per_seq),
                  pl.BlockSpec((rows, LANES), const2),
                  pl.BlockSpec((KEY_BLOCK, 2 * KEY_BLOCK), const2)]
        + [page_spec(i) for i in range(pp)] + [page_spec(i) for i in range(pp)],
        out_specs=pl.BlockSpec((1, t_new, d), per_seq),
        scratch_shapes=[pltpu.VMEM((rows, d), BF16),
                        pltpu.VMEM((PAGE_SIZE, d), F32),
                        pltpu.VMEM((PAGE_SIZE, d), F32),
                        pltpu.VMEM((rows, LANES), F32),
                        pltpu.VMEM((rows, d), F32)])
    return pl.pallas_call(
        functools.partial(_sb_sample_kernel, pp=pp, t_new=t_new, scale=SB_HEAD_DIM ** -0.5),
        grid_spec=grid_spec,
        out_shape=jax.ShapeDtypeStruct((n, t_new, d), F32),
        compiler_params=_params("arbitrary", "arbitrary"),
    )(page_table, q, k8, v8, bias_tile, _suffix_matrix(), *([cache_kt] * pp), *([cache_vt] * pp))


def _t5_bucket_np(dist):
    d = np.maximum(dist, 0)
    max_exact = REL_BUCKETS // 2
    large = max_exact + (np.log(np.maximum(d, 1).astype(np.float32) / np.float32(max_exact))
                         / np.float32(math.log(REL_MAX_DIST / max_exact))
                         * np.float32(REL_BUCKETS - max_exact)).astype(np.int32)
    large = np.minimum(large, REL_BUCKETS - 1)
    return np.where(d < max_exact, d, large).astype(np.int32)


def _swa_prompt_kernel(relb_ref, sinks_ref, q_ref, kp_ref, kc_ref, vp_ref, vc_ref, bucket_ref, o_ref, bias_ref,
                       *, scale):
    first = jnp.logical_and(pl.program_id(0) == 0, pl.program_id(1) == 0)
    n = pl.program_id(1)
    kvw = SWA_KV_HEADS * SWA_HEAD_DIM

    @pl.when(first)
    def _():
        bucket = bucket_ref[...]
        for h in range(SWA_HEADS):
            acc = jnp.zeros(bucket.shape, F32)
            for bk in range(REL_BUCKETS):
                acc = jnp.where(bucket == bk, relb_ref[bk * SWA_HEADS + h], acc)
            bias_ref[h] = acc

    shape = (WINDOW, 2 * WINDOW)
    row = lax.broadcasted_iota(jnp.int32, shape, 0)
    col = lax.broadcasted_iota(jnp.int32, shape, 1)
    dist = row - col + WINDOW
    valid = (dist >= 0) & (dist <= WINDOW) & ((col >= WINDOW) | (n > 0))
    kband = jnp.concatenate([kp_ref[0], kc_ref[0]], axis=1)
    vband = jnp.concatenate([vp_ref[0], vc_ref[0]], axis=1)
    lane = lax.broadcasted_iota(jnp.int32, (WINDOW, kvw), 1)
    for g in range(SWA_GROUP):
        qg = q_ref[0, :, g * kvw:(g + 1) * kvw]
        og = jnp.zeros((WINDOW, kvw), F32)
        for k in range(SWA_KV_HEADS):
            own = (lane // SWA_HEAD_DIM) == k
            qe = jnp.where(own, qg, jnp.zeros_like(qg))
            head = k * SWA_GROUP + g
            logits = _dot(qe, kband) * scale + bias_ref[head]
            logits = jnp.where(valid, logits, -jnp.inf)
            sink = sinks_ref[head]
            m = jnp.maximum(jnp.max(logits, axis=-1, keepdims=True), sink)
            pe = jnp.exp(logits - m)
            den = jnp.sum(pe, axis=-1, keepdims=True) + jnp.exp(sink - m)
            ok = _dot_nt((pe / den).astype(BF16), vband)
            og = jnp.where(own, ok, og)
        o_ref[0, :, g * kvw:(g + 1) * kvw] = og


def _swa_prompt_attn(q, kt, vt, rel_bias, sinks):
    b, s, d = q.shape
    kvw = kt.shape[1]
    nb = s // WINDOW
    dist = np.arange(WINDOW)[:, None] - np.arange(2 * WINDOW)[None, :] + WINDOW
    bucket = jnp.asarray(_t5_bucket_np(dist))
    prev = pl.BlockSpec((1, kvw, WINDOW), lambda bi, n: (bi, 0, jnp.maximum(n - 1, 0)))
    cur = pl.BlockSpec((1, kvw, WINDOW), lambda bi, n: (bi, 0, n))
    smem = pl.BlockSpec(memory_space=pltpu.SMEM)
    return pl.pallas_call(
        functools.partial(_swa_prompt_kernel, scale=SWA_HEAD_DIM ** -0.5),
        grid=(b, nb),
        in_specs=[smem, smem,
                  pl.BlockSpec((1, WINDOW, d), lambda bi, n: (bi, n, 0)),
                  prev, cur, prev, cur,
                  pl.BlockSpec((WINDOW, 2 * WINDOW), lambda bi, n: (0, 0))],
        out_specs=pl.BlockSpec((1, WINDOW, d), lambda bi, n: (bi, n, 0)),
        out_shape=jax.ShapeDtypeStruct((b, s, d), F32),
        scratch_shapes=[pltpu.VMEM((SWA_HEADS, WINDOW, 2 * WINDOW), F32)],
        compiler_params=_params("arbitrary", "arbitrary"),
    )(rel_bias.reshape(-1), sinks, q, kt, kt, vt, vt, bucket)


def _swa_sample_kernel(q_ref, ck_ref, cv_ref, kn_ref, vn_ref, sink_ref, relrow_ref, bucket_ref, o_ref,
                       bias_ref, kpad_ref, vpad_ref, *, t_new, scale):
    per_kv = t_new * SWA_GROUP
    rows = SWA_KV_HEADS * per_kv
    kvw = SWA_KV_HEADS * SWA_HEAD_DIM

    @pl.when(pl.program_id(0) == 0)
    def _():
        bucket = bucket_ref[...]
        acc = jnp.zeros(bucket.shape, F32)
        for bk in range(REL_BUCKETS):
            acc = jnp.where(bucket == bk, relrow_ref[:, bk:bk + 1], acc)
        bias_ref[...] = acc
        kpad_ref[...] = jnp.zeros_like(kpad_ref)
        vpad_ref[...] = jnp.zeros_like(vpad_ref)

    kpad_ref[0:SUBLANES, :] = kn_ref[0]
    vpad_ref[0:SUBLANES, :] = vn_ref[0]
    q16 = q_ref[0]
    lane_q = lax.broadcasted_iota(jnp.int32, (per_kv, kvw), 1)
    qexp = jnp.concatenate([jnp.where((lane_q // SWA_HEAD_DIM) == k, q16, 0.0) for k in range(SWA_KV_HEADS)],
                           axis=0).astype(BF16)
    ckt = ck_ref[0].reshape(kvw, WINDOW).astype(BF16)
    cvt = cv_ref[0].reshape(kvw, WINDOW).astype(BF16)
    zc = _dot(qexp, ckt) * scale + bias_ref[:, :WINDOW]
    zn = _dot_nt(qexp, kpad_ref[...].astype(BF16)) * scale + bias_ref[:, WINDOW:]
    row = lax.broadcasted_iota(jnp.int32, (rows, WINDOW), 0)
    key = lax.broadcasted_iota(jnp.int32, (rows, WINDOW), 1)
    t_row = (row % per_kv) // SWA_GROUP
    zc = jnp.where(key >= t_row, zc, -jnp.inf)
    zn = jnp.where(key <= t_row, zn, -jnp.inf)
    sink = sink_ref[:, 0:1]
    m = jnp.maximum(jnp.maximum(jnp.max(zc, axis=-1, keepdims=True), jnp.max(zn, axis=-1, keepdims=True)), sink)
    pc = jnp.exp(zc - m)
    pn = jnp.exp(zn - m)
    den = jnp.sum(pc, axis=-1, keepdims=True) + jnp.sum(pn, axis=-1, keepdims=True) + jnp.exp(sink - m)
    o = _dot_nt((pc / den).astype(BF16), cvt) + _dot((pn / den).astype(BF16), vpad_ref[...].astype(BF16))
    lane_o = lax.broadcasted_iota(jnp.int32, (per_kv, kvw), 1)
    res = jnp.zeros((per_kv, kvw), F32)
    for k in range(SWA_KV_HEADS):
        res = jnp.where((lane_o // SWA_HEAD_DIM) == k, o[k * per_kv:(k + 1) * per_kv, :], res)
    o_ref[0] = res


def _swa_sample_attn(q, k_new, v_new, cache_kt, cache_vt, rel_bias, sinks):
    n, t_new, d = q.shape
    kvw = k_new.shape[-1]
    per_kv = t_new * SWA_GROUP
    rows = SWA_KV_HEADS * per_kv
    pad = ((0, 0), (0, SUBLANES - t_new), (0, 0))
    r = np.arange(rows)
    t_of_row = (r % per_kv) // SWA_GROUP
    head_of_row = (r // per_kv) * SWA_GROUP + (r % SWA_GROUP)
    c = np.arange(2 * WINDOW)
    dist = np.where(c[None, :] < WINDOW, t_of_row[:, None] + WINDOW - c[None, :],
                    t_of_row[:, None] - (c[None, :] - WINDOW))
    bucket = jnp.asarray(_t5_bucket_np(dist))
    relrow = rel_bias.T[head_of_row]
    sink_tile = jnp.broadcast_to(sinks[head_of_row][:, None], (rows, LANES))
    per_seq = lambda s: (s, 0, 0)
    per_seq4 = lambda s: (s, 0, 0, 0)
    const2 = lambda s: (0, 0)
    cache_spec = pl.BlockSpec((1, SWA_KV_HEADS, SWA_HEAD_DIM, WINDOW), per_seq4)
    out = pl.pallas_call(
        functools.partial(_swa_sample_kernel, t_new=t_new, scale=SWA_HEAD_DIM ** -0.5),
        grid=(n,),
        in_specs=[pl.BlockSpec((1, per_kv, kvw), per_seq),
                  cache_spec, cache_spec,
                  pl.BlockSpec((1, SUBLANES, kvw), per_seq),
                  pl.BlockSpec((1, SUBLANES, kvw), per_seq),
                  pl.BlockSpec((rows, LANES), const2),
                  pl.BlockSpec((rows, REL_BUCKETS), const2),
                  pl.BlockSpec((rows, 2 * WINDOW), const2)],
        out_specs=pl.BlockSpec((1, per_kv, kvw), per_seq),
        out_shape=jax.ShapeDtypeStruct((n, per_kv, kvw), F32),
        scratch_shapes=[pltpu.VMEM((rows, 2 * WINDOW), F32),
                        pltpu.VMEM((WINDOW, kvw), F32),
                        pltpu.VMEM((WINDOW, kvw), F32)],
        compiler_params=_params("arbitrary"),
    )(q.reshape(n, per_kv, kvw), cache_kt, cache_vt, jnp.pad(k_new, pad), jnp.pad(v_new, pad),
      sink_tile, relrow, bucket)
    return out.reshape(n, t_new, d)


def _conv_post(acc, gate, lng, lnb):
    if gate is not None:
        return gate * acc
    y = _layer_norm(acc, lng, lnb)
    return y * jax.nn.sigmoid(y)


def _conv_prompt_kernel(u_ref, halo_ref, *rest, width, halo, gated):
    if gated:
        gate_ref, w_ref, b_ref, y_ref, ext_ref = rest
    else:
        w_ref, b_ref, lng_ref, lnb_ref, y_ref, ext_ref = rest
    ts = u_ref.shape[1]
    c = u_ref.shape[2]
    s = pl.program_id(1)
    ext_ref[0:halo, :] = jnp.where(s > 0, halo_ref[0], 0.0)
    ext_ref[halo:halo + ts, :] = u_ref[0]
    base = halo - (width - 1)
    for r0 in range(0, ts, CONV_ROWS):
        acc = jnp.broadcast_to(b_ref[...], (CONV_ROWS, c))
        for k in range(width):
            acc = acc + w_ref[k:k + 1, :] * ext_ref[r0 + base + k:r0 + base + k + CONV_ROWS, :]
        if gated:
            y = _conv_post(acc, gate_ref[0, r0:r0 + CONV_ROWS, :], None, None)
        else:
            y = _conv_post(acc, None, lng_ref[...], lnb_ref[...])
        y_ref[0, r0:r0 + CONV_ROWS, :] = y


def _conv_prompt(u, w, b, gate=None, lng=None, lnb=None):
    bsz, s, c = u.shape
    width = w.shape[0]
    halo = SUBLANES * (-(-(width - 1) // SUBLANES))
    ts = min(CONV_TILE, s)
    per_tile = ts // halo
    tile = pl.BlockSpec((1, ts, c), lambda bi, si: (bi, si, 0))
    vec = pl.BlockSpec((1, c), lambda bi, si: (0, 0))
    in_specs = [tile, pl.BlockSpec((1, halo, c), lambda bi, si: (bi, jnp.maximum(si * per_tile - 1, 0), 0))]
    args = [u, u]
    if gate is not None:
        in_specs += [tile]
        args += [gate]
    in_specs += [pl.BlockSpec((width, c), lambda bi, si: (0, 0)), vec]
    args += [w, b.reshape(1, c)]
    if gate is None:
        in_specs += [vec, vec]
        args += [lng.reshape(1, c), lnb.reshape(1, c)]
    return pl.pallas_call(
        functools.partial(_conv_prompt_kernel, width=width, halo=halo, gated=gate is not None),
        grid=(bsz, s // ts),
        in_specs=in_specs,
        out_specs=tile,
        out_shape=jax.ShapeDtypeStruct((bsz, s, c), F32),
        scratch_shapes=[pltpu.VMEM((halo + ts, c), F32)],
        compiler_params=_params("arbitrary", "arbitrary"),
    )(*args)


def _conv_sample_kernel(full_ref, *rest, width, gated):
    if gated:
        gate_ref, w_ref, b_ref, y_ref = rest
    else:
        w_ref, b_ref, lng_ref, lnb_ref, y_ref = rest
    t_new, nseq, c = y_ref.shape
    for t in range(t_new):
        acc = jnp.broadcast_to(b_ref[...], (nseq, c))
        for k in range(width):
            acc = acc + w_ref[k:k + 1, :] * full_ref[t + k]
        if gated:
            y = _conv_post(acc, gate_ref[t], None, None)
        else:
            y = _conv_post(acc, None, lng_ref[...], lnb_ref[...])
        y_ref[t] = y


def _conv_sample(full, w, b, gate=None, lng=None, lnb=None):
    rows, n, c = full.shape
    width = w.shape[0]
    t_new = rows - (width - 1)
    ns = min(CONV_SEQS, n)
    seqs = lambda r: pl.BlockSpec((r, ns, c), lambda i: (0, i, 0))
    vec = pl.BlockSpec((1, c), lambda i: (0, 0))
    in_specs = [seqs(rows)]
    args = [full]
    if gate is not None:
        in_specs += [seqs(t_new)]
        args += [gate]
    in_specs += [pl.BlockSpec((width, c), lambda i: (0, 0)), vec]
    args += [w, b.reshape(1, c)]
    if gate is None:
        in_specs += [vec, vec]
        args += [lng.reshape(1, c), lnb.reshape(1, c)]
    return pl.pallas_call(
        functools.partial(_conv_sample_kernel, width=width, gated=gate is not None),
        grid=(n // ns,),
        in_specs=in_specs,
        out_specs=seqs(t_new),
        out_shape=jax.ShapeDtypeStruct((t_new, n, c), F32),
        compiler_params=_params("arbitrary"),
    )(*args)


def _cols(w, n):
    return [c.astype(BF16) for c in jnp.split(w, n, axis=1)]


def _time_major(x):
    return jnp.transpose(x, (1, 0, 2))


def _heads_last(xt, heads, head_dim):
    g, _, s = xt.shape
    return jnp.transpose(xt.reshape(g, heads, head_dim, s), (0, 3, 1, 2))


def _positions_last(x):
    return jnp.transpose(x, (0, 2, 3, 1))


def kernel(x_prompt, x_sample, cache_sb_k, cache_sb_v, page_table, state_gc_conv, cache_swa_k, cache_swa_v, state_cf_conv, sb_w_qkv, sb_bias, sb_w_o, gc_w_in, gc_conv_w, gc_conv_b, gc_w_out, swa_w_qkv, swa_sinks, swa_w_o, rel_bias, cf_w_in, cf_conv_w, cf_conv_b, cf_ln_g, cf_ln_b, cf_w_out, ffn_w_gate, ffn_w_up, ffn_w_down, ln_g, ln_b):
    bsz, seq, d = x_prompt.shape
    n, t_new, _ = x_sample.shape
    m_p, m_s = bsz * seq, n * t_new
    xp = x_prompt.reshape(m_p, d)
    xs = x_sample.reshape(m_s, d)

    def ln_params(i, j):
        return ln_g[i, j].reshape(1, d), ln_b[i, j].reshape(1, d)

    def ffn(x, i):
        g, b = ln_params(i, 1)
        return _ffn_ln(x, ffn_w_gate[i].astype(BF16), ffn_w_up[i].astype(BF16), ffn_w_down[i].astype(BF16), g, b)

    def as_p(x):
        return x.reshape(bsz, seq, d)

    def as_s(x):
        return x.reshape(1, m_s, d)

    g0, b0 = ln_params(0, 0)
    w_q, w_k, w_v = _cols(sb_w_qkv, 3)
    w_sb_o = sb_w_o.astype(BF16)
    qp, kt_p, vt_p, kt_b, vt_b = _mm(
        as_p(xp), [w_q], [w_k.T, w_v.T], lambda ps, ts: (ps[0], ts[0], ts[1], ts[0], ts[1]),
        [('n', BF16), ('t', F32), ('t', F32), ('t', BF16), ('t', BF16)])
    o = _sb_prompt_attn(qp, kt_b, vt_b, sb_bias)
    xp = _proj_ln(o.reshape(m_p, d), w_sb_o, xp, g0, b0)
    sb_k_p = _heads_last(kt_p, SB_HEADS, SB_HEAD_DIM)
    sb_v_p = _heads_last(vt_p, SB_HEADS, SB_HEAD_DIM)
    qs, k_s, v_s = _mm(as_s(xs), [w_q, w_k, w_v], [], lambda ps, ts: ps, [('n', F32)] * 3)
    qs, k_s, v_s = (a.reshape(n, t_new, d) for a in (qs, k_s, v_s))
    o = _sb_sample_attn(qs, k_s, v_s, _positions_last(cache_sb_k), _positions_last(cache_sb_v), page_table, sb_bias)
    xs = _proj_ln(o.reshape(m_s, d), w_sb_o, xs, g0, b0)
    sb_k_s = k_s.reshape(n, t_new, SB_HEADS, SB_HEAD_DIM)
    sb_v_s = v_s.reshape(n, t_new, SB_HEADS, SB_HEAD_DIM)
    xp, xs = ffn(xp, 0), ffn(xs, 0)

    g1, b1 = ln_params(1, 0)
    w_gc = _cols(gc_w_in, 3)
    w_gc_o = gc_w_out.astype(BF16)
    gc_epilogue = lambda ps, ts: (ps[0], ps[1] * ps[2])
    gc_outs = [('n', F32), ('n', F32)]
    gate_p, u_p = _mm(as_p(xp), w_gc, [], gc_epilogue, gc_outs)
    y = _conv_prompt(u_p, gc_conv_w, gc_conv_b, gate=gate_p)
    xp = _proj_ln(y.reshape(m_p, d), w_gc_o, xp, g1, b1)
    gc_p = u_p[:, seq - (GC_WIDTH - 1):]
    gate_s, u_s = _mm(as_s(xs), w_gc, [], gc_epilogue, gc_outs)
    full = jnp.concatenate([_time_major(state_gc_conv), _time_major(u_s.reshape(n, t_new, d))], axis=0)
    y = _conv_sample(full, gc_conv_w, gc_conv_b, gate=_time_major(gate_s.reshape(n, t_new, d)))
    xs = _proj_ln(_time_major(y).reshape(m_s, d), w_gc_o, xs, g1, b1)
    gc_s = _time_major(full[full.shape[0] - (GC_WIDTH - 1):])
    xp, xs = ffn(xp, 1), ffn(xs, 1)

    g2, b2 = ln_params(2, 0)
    qd = SWA_HEADS * SWA_HEAD_DIM
    kvw = SWA_KV_HEADS * SWA_HEAD_DIM
    w_q = swa_w_qkv[:, :qd].reshape(d, SWA_KV_HEADS, SWA_GROUP, SWA_HEAD_DIM)
    w_q = w_q.transpose(0, 2, 1, 3).reshape(d, qd).astype(BF16)
    w_k, w_v = _cols(swa_w_qkv[:, qd:], 2)
    w_swa_o = swa_w_o.reshape(SWA_KV_HEADS, SWA_GROUP, SWA_HEAD_DIM, d).transpose(1, 0, 2, 3)
    w_swa_o = w_swa_o.reshape(qd, d).astype(BF16)
    (qp,) = _mm(as_p(xp), [w_q], [], lambda ps, ts: ps, [('n', BF16)])
    kt_p, vt_p, kt_b, vt_b = _mm(
        as_p(xp), [], [w_k.T, w_v.T], lambda ps, ts: (ts[0], ts[1], ts[0], ts[1]),
        [('t', F32), ('t', F32), ('t', BF16), ('t', BF16)], tn=kvw)
    o = _swa_prompt_attn(qp, kt_b, vt_b, rel_bias, swa_sinks)
    xp = _proj_ln(o.reshape(m_p, qd), w_swa_o, xp, g2, b2)
    buf = min(WINDOW, seq)
    swa_k_p = _heads_last(kt_p[:, :, seq - buf:], SWA_KV_HEADS, SWA_HEAD_DIM)
    swa_v_p = _heads_last(vt_p[:, :, seq - buf:], SWA_KV_HEADS, SWA_HEAD_DIM)
    (qs,) = _mm(as_s(xs), [w_q], [], lambda ps, ts: ps, [('n', F32)])
    k_s, v_s = _mm(as_s(xs), [w_k, w_v], [], lambda ps, ts: ps, [('n', F32)] * 2, tn=kvw)
    k_s, v_s = k_s.reshape(n, t_new, kvw), v_s.reshape(n, t_new, kvw)
    o = _swa_sample_attn(qs.reshape(n, t_new, qd), k_s, v_s, _positions_last(cache_swa_k),
                         _positions_last(cache_swa_v), rel_bias, swa_sinks)
    xs = _proj_ln(o.reshape(m_s, qd), w_swa_o, xs, g2, b2)
    kv_heads = (SWA_KV_HEADS, SWA_HEAD_DIM)
    swa_k_s = jnp.concatenate([cache_swa_k, k_s.reshape(n, t_new, *kv_heads)], axis=1)[:, t_new:]
    swa_v_s = jnp.concatenate([cache_swa_v, v_s.reshape(n, t_new, *kv_heads)], axis=1)[:, t_new:]
    xp, xs = ffn(xp, 2), ffn(xs, 2)

    g3, b3 = ln_params(3, 0)
    w_cf = _cols(cf_w_in, 2)
    w_cf_o = cf_w_out.astype(BF16)
    cf_epilogue = lambda ps, ts: (ps[0] * jax.nn.sigmoid(ps[1]),)
    (u_p,) = _mm(as_p(xp), w_cf, [], cf_epilogue, [('n', F32)])
    y = _conv_prompt(u_p, cf_conv_w, cf_conv_b, lng=cf_ln_g, lnb=cf_ln_b)
    xp = _proj_ln(y.reshape(m_p, d), w_cf_o, xp, g3, b3)
    cf_p = u_p[:, seq - (CF_WIDTH - 1):]
    (u_s,) = _mm(as_s(xs), w_cf, [], cf_epilogue, [('n', F32)])
    full = jnp.concatenate([_time_major(state_cf_conv), _time_major(u_s.reshape(n, t_new, d))], axis=0)
    y = _conv_sample(full, cf_conv_w, cf_conv_b, lng=cf_ln_g, lnb=cf_ln_b)
    xs = _proj_ln(_time_major(y).reshape(m_s, d), w_cf_o, xs, g3, b3)
    cf_s = _time_major(full[full.shape[0] - (CF_WIDTH - 1):])
    xp, xs = ffn(xp, 3), ffn(xs, 3)

    return (xp.reshape(bsz, seq, d), xs.reshape(n, t_new, d), sb_k_p, sb_v_p, sb_k_s, sb_v_s,
            gc_p, gc_s, swa_k_p, swa_v_p, swa_k_s, swa_v_s, cf_p, cf_s)
```

```python
import functools
import math

import numpy as np
import jax
import jax.numpy as jnp
from jax import lax
from jax.experimental import pallas as pl
from jax.experimental.pallas import tpu as pltpu

F32 = jnp.float32
BF16 = jnp.bfloat16

DEPTH = 4
PAGE_SIZE = 128
SB_HEADS = 16
SB_HEAD_DIM = 64
SB_SCALE = SB_HEAD_DIM ** -0.5
assert math.frexp(SB_SCALE)[0] == 0.5
GC_WIDTH = 3
SWA_HEADS = 16
SWA_KV_HEADS = 4
SWA_HEAD_DIM = 64
SWA_GROUP = SWA_HEADS // SWA_KV_HEADS
WINDOW = 128
REL_BUCKETS = 32
REL_MAX_DIST = 128
CF_WIDTH = 31
DEEP_ALPHA = (2 * DEPTH) ** 0.25
LN_EPS = 1e-5

LANES = 128
SUBLANES = 8
VMEM_LIMIT = 56 * 1024 * 1024
TM = 512
TN = 512
FFN_STEPS = 2
KEY_BLOCK = 128
SB_ROWS = 512
SB_PAGES_PER_STEP = 8
CONV_ROWS = 16
CONV_TILE = 128
CONV_SEQS = 16

NT_DIMS = (((1,), (1,)), ((), ()))


def _params(*sem):
    return pltpu.CompilerParams(dimension_semantics=sem, vmem_limit_bytes=VMEM_LIMIT)


def _dot(a, b):
    return jnp.dot(a, b, preferred_element_type=F32)


def _dot_nt(a, b):
    return lax.dot_general(a, b, NT_DIMS, preferred_element_type=F32)


def _layer_norm(x, g, b):
    mu = jnp.mean(x, axis=-1, keepdims=True)
    xc = x - mu
    var = jnp.mean(xc * xc, axis=-1, keepdims=True)
    return xc * lax.rsqrt(var + LN_EPS) * g + b


def _mm_kernel(x_ref, *refs, n_w, n_wt, epilogue):
    w_refs, wt_refs, out_refs = refs[:n_w], refs[n_w:n_w + n_wt], refs[n_w + n_wt:]
    xb = x_ref[0].astype(BF16)
    parts = [_dot(xb, w[...]) for w in w_refs]
    tparts = [_dot_nt(wt[...], xb) for wt in wt_refs]
    for o_ref, val in zip(out_refs, epilogue(parts, tparts)):
        o_ref[0] = val.astype(o_ref.dtype)


def _mm(x, ws, wts, epilogue, outs, tn=TN, name="proj"):
    g, s, k = x.shape
    n = ws[0].shape[1] if ws else wts[0].shape[0]
    tm = min(TM, s)
    out_specs, out_shape = [], []
    for kind, dt in outs:
        if kind == 'n':
            out_specs.append(pl.BlockSpec((1, tm, tn), lambda gi, i, j: (gi, i, j)))
            out_shape.append(jax.ShapeDtypeStruct((g, s, n), dt))
        else:
            out_specs.append(pl.BlockSpec((1, tn, tm), lambda gi, i, j: (gi, j, i)))
            out_shape.append(jax.ShapeDtypeStruct((g, n, s), dt))
    return pl.pallas_call(
        functools.partial(_mm_kernel, n_w=len(ws), n_wt=len(wts), epilogue=epilogue),
        grid=(g, s // tm, n // tn),
        in_specs=[pl.BlockSpec((1, tm, k), lambda gi, i, j: (gi, i, 0))]
        + [pl.BlockSpec((k, tn), lambda gi, i, j: (0, j)) for _ in ws]
        + [pl.BlockSpec((tn, k), lambda gi, i, j: (j, 0)) for _ in wts],
        out_specs=out_specs,
        out_shape=out_shape,
        compiler_params=_params("arbitrary", "arbitrary", "arbitrary"),
        name=name,
    )(x, *ws, *wts)


def _proj_ln_kernel(a_ref, w_ref, x_ref, g_ref, b_ref, o_ref):
    d = _dot(a_ref[...].astype(BF16), w_ref[...])
    o_ref[...] = _layer_norm(DEEP_ALPHA * x_ref[...] + d, g_ref[...], b_ref[...])


def _proj_ln(a, w, x, g, b):
    m, k = a.shape
    d = x.shape[1]
    tm = min(TM, m)
    return pl.pallas_call(
        _proj_ln_kernel,
        grid=(m // tm,),
        in_specs=[pl.BlockSpec((tm, k), lambda i: (i, 0)),
                  pl.BlockSpec((k, d), lambda i: (0, 0)),
                  pl.BlockSpec((tm, d), lambda i: (i, 0)),
                  pl.BlockSpec((1, d), lambda i: (0, 0)),
                  pl.BlockSpec((1, d), lambda i: (0, 0))],
        out_specs=pl.BlockSpec((tm, d), lambda i: (i, 0)),
        out_shape=jax.ShapeDtypeStruct((m, d), F32),
        compiler_params=_params("arbitrary"),
        name="proj_ln",
    )(a, w, x, g, b)


def _ffn_kernel(x_ref, wg_ref, wu_ref, wd_ref, g_ref, b_ref, o_ref, acc_ref, xb_ref):
    f = pl.program_id(1)

    @pl.when(f == 0)
    def _():
        xb_ref[...] = x_ref[...].astype(BF16)
        acc_ref[...] = jnp.zeros_like(acc_ref)

    xb = xb_ref[...]
    hg = _dot(xb, wg_ref[...])
    hu = _dot(xb, wu_ref[...])
    h = (hg * jax.nn.sigmoid(hg)) * hu
    acc_ref[...] += _dot(h.astype(BF16), wd_ref[...])

    @pl.when(f == pl.num_programs(1) - 1)
    def _():
        o_ref[...] = _layer_norm(DEEP_ALPHA * x_ref[...] + acc_ref[...], g_ref[...], b_ref[...])


def _ffn_ln(x, wg, wu, wd, g, b):
    m, d = x.shape
    d_ff = wg.shape[1]
    tf = d_ff // FFN_STEPS
    tm = min(TM, m)
    return pl.pallas_call(
        _ffn_kernel,
        grid=(m // tm, FFN_STEPS),
        in_specs=[pl.BlockSpec((tm, d), lambda i, f: (i, 0)),
                  pl.BlockSpec((d, tf), lambda i, f: (0, f)),
                  pl.BlockSpec((d, tf), lambda i, f: (0, f)),
                  pl.BlockSpec((tf, d), lambda i, f: (f, 0)),
                  pl.BlockSpec((1, d), lambda i, f: (0, 0)),
                  pl.BlockSpec((1, d), lambda i, f: (0, 0))],
        out_specs=pl.BlockSpec((tm, d), lambda i, f: (i, 0)),
        out_shape=jax.ShapeDtypeStruct((m, d), F32),
        scratch_shapes=[pltpu.VMEM((tm, d), F32), pltpu.VMEM((tm, d), BF16)],
        compiler_params=_params("arbitrary", "arbitrary"),
        name="ffn_ln",
    )(x, wg, wu, wd, g, b)


def _suffix_matrices(nblk):
    j = np.arange(KEY_BLOCK)[:, None]
    s = np.arange(KEY_BLOCK)[None, :]
    eye = np.eye(nblk, dtype=np.float32)
    after = np.kron(eye, (j > s).astype(np.float32))
    total = np.kron(eye, np.ones((KEY_BLOCK, KEY_BLOCK), np.float32))
    return jnp.asarray(after, dtype=BF16), jnp.asarray(total, dtype=BF16)


def _pair_suffix_matrices():
    j = np.arange(KEY_BLOCK)[:, None]
    s = np.arange(KEY_BLOCK)[None, :]
    strict = (j > s).astype(np.float32)
    one = np.ones_like(strict)
    after = np.block([[strict, one], [np.zeros_like(strict), strict]])
    return jnp.asarray(after, dtype=BF16), jnp.ones((2 * KEY_BLOCK, 2 * KEY_BLOCK), BF16)


def _sb_logs(z):
    softplus = jnp.maximum(z, 0.0) + jnp.log(1.0 + jnp.exp(-jnp.abs(z)))
    return softplus, z - softplus


def _sb_prompt_kernel(bias_ref, q_ref, kt_ref, vt_ref, uafter_ref, utotal_ref, o_ref, carry_ref, *, seq, rows):
    p = pl.program_id(1)
    j = pl.program_id(2)
    kb = pl.num_programs(2) - 1 - j
    two = 2 * KEY_BLOCK

    @pl.when(j == 0)
    def _():
        o_ref[...] = jnp.zeros_like(o_ref)
        carry_ref[...] = jnp.zeros_like(carry_ref)

    kt = kt_ref[0]
    vt = vt_ref[0]
    head0 = lax.broadcasted_iota(jnp.int32, kt.shape, 0) < SB_HEAD_DIM
    zero = jnp.zeros_like(kt)
    w_qk = jnp.concatenate([jnp.where(head0, kt, zero), jnp.where(head0, zero, kt)], axis=1)
    w_pv = jnp.concatenate([jnp.where(head0, vt, zero), jnp.where(head0, zero, vt)], axis=1)
    lane = lax.broadcasted_iota(jnp.int32, (1, two), 1)
    bias = jnp.where(lane < KEY_BLOCK, bias_ref[2 * p], bias_ref[2 * p + 1])
    key_minus_row = (lax.broadcasted_iota(jnp.int32, (rows, two), 1) % KEY_BLOCK
                     - lax.broadcasted_iota(jnp.int32, (rows, two), 0))

    def chunk(c, masked):
        r0 = pl.multiple_of(c * rows, rows)
        z = _dot(q_ref[0, pl.ds(r0, rows), :], w_qk) + bias
        softplus, log_sig = _sb_logs(z)
        if masked:
            visible = key_minus_row < (r0 - kb * KEY_BLOCK)
            softplus = jnp.where(visible, softplus, 0.0)
        sp = softplus.astype(BF16)
        later = carry_ref[pl.ds(r0, rows), :]
        a = jnp.exp(log_sig - (_dot(sp, uafter_ref[...]) + later))
        if masked:
            a = jnp.where(visible, a, 0.0)
        carry_ref[pl.ds(r0, rows), :] = later + _dot(sp, utotal_ref[...])
        o_ref[0, pl.ds(r0, rows), :] += _dot_nt(a.astype(BF16), w_pv)

    first = (kb * KEY_BLOCK) // rows
    chunk(first, True)

    def body(c, carry_unused):
        chunk(c, False)
        return carry_unused

    lax.fori_loop(first + 1, seq // rows, body, 0)


def _sb_prompt_attn(q, kt, vt, sb_bias):
    b, s, d = q.shape
    nkb = s // KEY_BLOCK
    rows = min(SB_ROWS, s)
    kv_spec = pl.BlockSpec((1, LANES, KEY_BLOCK), lambda bi, p, j: (bi, p, nkb - 1 - j))
    const = pl.BlockSpec((2 * KEY_BLOCK, 2 * KEY_BLOCK), lambda bi, p, j: (0, 0))
    return pl.pallas_call(
        functools.partial(_sb_prompt_kernel, seq=s, rows=rows),
        grid=(b, d // LANES, nkb),
        in_specs=[pl.BlockSpec(memory_space=pltpu.SMEM),
                  pl.BlockSpec((1, s, LANES), lambda bi, p, j: (bi, 0, p)),
                  kv_spec, kv_spec, const, const],
        out_specs=pl.BlockSpec((1, s, LANES), lambda bi, p, j: (bi, 0, p)),
        out_shape=jax.ShapeDtypeStruct((b, s, d), F32),
        scratch_shapes=[pltpu.VMEM((s, 2 * KEY_BLOCK), F32)],
        compiler_params=_params("arbitrary", "arbitrary", "arbitrary"),
        name="sb_prompt_attn",
    )(sb_bias, q, kt, vt, *_suffix_matrices(2))


def _sb_sample_kernel(pt_ref, q_ref, kn_ref, vn_ref, bias_ref, u2_ref, uafter_ref, utotal_ref, *rest,
                      pp, t_new, scale):
    del pt_ref
    k_refs, v_refs = rest[:pp], rest[pp:2 * pp]
    o_ref = rest[2 * pp]
    qbd_ref, kpad_ref, vpad_ref, carry_ref, acc_ref = rest[2 * pp + 1:]
    j = pl.program_id(1)
    rows = t_new * SB_HEADS
    d = q_ref.shape[-1]
    u2 = u2_ref[...]
    bias = bias_ref[...]

    row_d = lax.broadcasted_iota(jnp.int32, (rows, d), 0)
    lane_d = lax.broadcasted_iota(jnp.int32, (rows, d), 1)
    own_head = (lane_d // SB_HEAD_DIM) == (row_d % SB_HEADS)

    def block_terms(z, mask):
        softplus, log_sig = _sb_logs(z + bias)
        if mask is not None:
            softplus = jnp.where(mask, softplus, 0.0)
        sums = _dot(softplus.astype(BF16), u2)
        return log_sig, sums[:, :KEY_BLOCK], sums[:, KEY_BLOCK:]

    @pl.when(j == 0)
    def _():
        qv = q_ref[0] * scale
        qrep = jnp.concatenate([jnp.broadcast_to(qv[t:t + 1, :], (SB_HEADS, d)) for t in range(t_new)], axis=0)
        qbd_ref[...] = jnp.where(own_head, qrep, 0.0).astype(BF16)
        kpad_ref[...] = jnp.zeros_like(kpad_ref)
        vpad_ref[...] = jnp.zeros_like(vpad_ref)
        kpad_ref[0:SUBLANES, :] = kn_ref[0]
        vpad_ref[0:SUBLANES, :] = vn_ref[0]
        row = lax.broadcasted_iota(jnp.int32, (rows, LANES), 0)
        lane = lax.broadcasted_iota(jnp.int32, (rows, LANES), 1)
        visible = lane < row // SB_HEADS
        log_sig, after, total = block_terms(_dot_nt(qbd_ref[...], kpad_ref[...].astype(BF16)), visible)
        a = jnp.where(visible, jnp.exp(log_sig - after), 0.0)
        carry_ref[...] = total
        acc_ref[...] = _dot(a.astype(BF16), vpad_ref[...].astype(BF16))

    def page(refs, i):
        return refs[i][0].reshape(d, PAGE_SIZE).astype(BF16)

    qbd = qbd_ref[...]
    bias2 = jnp.concatenate([bias] * 2, axis=1)
    pairs = range(0, pp, 2)
    zs = [_dot(qbd, jnp.concatenate([page(k_refs, i), page(k_refs, i + 1)], axis=1)) for i in pairs]
    logs = [_sb_logs(z + bias2) for z in zs]
    sps = [softplus.astype(BF16) for softplus, _ in logs]
    afters = [_dot(sp, uafter_ref[...]) for sp in sps]
    totals = [_dot(sp, utotal_ref[...]) for sp in sps]
    later = jnp.concatenate([carry_ref[...]] * 2, axis=1)
    weights = []
    for (_, log_sig), after, total in zip(logs, afters, totals):
        weights.append(jnp.exp(log_sig - (after + later)).astype(BF16))
        later = later + total
    acc = acc_ref[...]
    for i, a in zip(pairs, weights):
        acc = acc + _dot_nt(a, jnp.concatenate([page(v_refs, i), page(v_refs, i + 1)], axis=1))
    carry_ref[...] = later[:, :KEY_BLOCK]
    acc_ref[...] = acc

    @pl.when(j == pl.num_programs(1) - 1)
    def _():
        acc = jnp.where(own_head, acc_ref[...], 0.0)
        o_ref[0] = jnp.concatenate(
            [jnp.sum(acc[t * SB_HEADS:(t + 1) * SB_HEADS, :], axis=0, keepdims=True) for t in range(t_new)], axis=0)


def _sb_sample_attn(q, k_new, v_new, cache_kt, cache_vt, page_table, sb_bias):
    n, t_new, d = q.shape
    n_pages = page_table.shape[1]
    pp = min(SB_PAGES_PER_STEP, n_pages)
    assert pp % 2 == 0 and n_pages % pp == 0
    rows = t_new * SB_HEADS
    pad = ((0, 0), (0, SUBLANES - t_new), (0, 0))
    k8, v8 = jnp.pad(k_new, pad), jnp.pad(v_new, pad)
    bias_tile = jnp.broadcast_to(jnp.tile(sb_bias, t_new)[:, None], (rows, LANES))

    def page_spec(i):
        return pl.BlockSpec((1, SB_HEADS, SB_HEAD_DIM, PAGE_SIZE),
                            lambda s, j, pt: (pt[s, n_pages - 1 - (j * pp + i)], 0, 0, 0))

    per_seq = lambda s, j, pt: (s, 0, 0)
    const2 = lambda s, j, pt: (0, 0)
    grid_spec = pltpu.PrefetchScalarGridSpec(
        num_scalar_prefetch=1,
        grid=(n, n_pages // pp),
        in_specs=[pl.BlockSpec((1, t_new, d), per_seq),
                  pl.BlockSpec((1, SUBLANES, d), per_seq),
                  pl.BlockSpec((1, SUBLANES, d), per_seq),
                  pl.BlockSpec((rows, LANES), const2),
                  pl.BlockSpec((KEY_BLOCK, 2 * KEY_BLOCK), const2),
                  pl.BlockSpec((2 * KEY_BLOCK, 2 * KEY_BLOCK), const2),
                  pl.BlockSpec((2 * KEY_BLOCK, 2 * KEY_BLOCK), const2)]
        + [page_spec(i) for i in range(pp)] + [page_spec(i) for i in range(pp)],
        out_specs=pl.BlockSpec((1, t_new, d), per_seq),
        scratch_shapes=[pltpu.VMEM((rows, d), BF16),
                        pltpu.VMEM((PAGE_SIZE, d), F32),
                        pltpu.VMEM((PAGE_SIZE, d), F32),
                        pltpu.VMEM((rows, LANES), F32),
                        pltpu.VMEM((rows, d), F32)])
    return pl.pallas_call(
        functools.partial(_sb_sample_kernel, pp=pp, t_new=t_new, scale=SB_SCALE),
        grid_spec=grid_spec,
        out_shape=jax.ShapeDtypeStruct((n, t_new, d), F32),
        compiler_params=_params("arbitrary", "arbitrary"),
        name="sb_sample_attn",
    )(page_table, q, k8, v8, bias_tile, jnp.concatenate(_suffix_matrices(1), axis=1), *_pair_suffix_matrices(),
      *([cache_kt] * pp), *([cache_vt] * pp))


def _t5_bucket_np(dist):
    d = np.maximum(dist, 0)
    max_exact = REL_BUCKETS // 2
    large = max_exact + (np.log(np.maximum(d, 1).astype(np.float32) / np.float32(max_exact))
                         / np.float32(math.log(REL_MAX_DIST / max_exact))
                         * np.float32(REL_BUCKETS - max_exact)).astype(np.int32)
    large = np.minimum(large, REL_BUCKETS - 1)
    return np.where(d < max_exact, d, large).astype(np.int32)


def _swa_prompt_kernel(relb_ref, sinks_ref, q_ref, kp_ref, kc_ref, vp_ref, vc_ref, bucket_ref, o_ref, bias_ref,
                       *, scale):
    first = jnp.logical_and(pl.program_id(0) == 0, pl.program_id(1) == 0)
    n = pl.program_id(1)
    kvw = SWA_KV_HEADS * SWA_HEAD_DIM

    @pl.when(first)
    def _():
        bucket = bucket_ref[...]
        for h in range(SWA_HEADS):
            acc = jnp.zeros(bucket.shape, F32)
            for bk in range(REL_BUCKETS):
                acc = jnp.where(bucket == bk, relb_ref[bk * SWA_HEADS + h], acc)
            bias_ref[h] = acc

    shape = (WINDOW, 2 * WINDOW)
    row = lax.broadcasted_iota(jnp.int32, shape, 0)
    col = lax.broadcasted_iota(jnp.int32, shape, 1)
    dist = row - col + WINDOW
    valid = (dist >= 0) & (dist <= WINDOW) & ((col >= WINDOW) | (n > 0))
    kband = jnp.concatenate([kp_ref[0], kc_ref[0]], axis=1)
    vband = jnp.concatenate([vp_ref[0], vc_ref[0]], axis=1)
    lane = lax.broadcasted_iota(jnp.int32, (WINDOW, kvw), 1)
    for g in range(SWA_GROUP):
        qg = q_ref[0, :, g * kvw:(g + 1) * kvw]
        og = jnp.zeros((WINDOW, kvw), F32)
        for k in range(SWA_KV_HEADS):
            own = (lane // SWA_HEAD_DIM) == k
            qe = jnp.where(own, qg, jnp.zeros_like(qg))
            head = k * SWA_GROUP + g
            logits = _dot(qe, kband) * scale + bias_ref[head]
            logits = jnp.where(valid, logits, -jnp.inf)
            sink = sinks_ref[head]
            m = jnp.maximum(jnp.max(logits, axis=-1, keepdims=True), sink)
            pe = jnp.exp(logits - m)
            den = jnp.sum(pe, axis=-1, keepdims=True) + jnp.exp(sink - m)
            ok = _dot_nt((pe / den).astype(BF16), vband)
            og = jnp.where(own, ok, og)
        o_ref[0, :, g * kvw:(g + 1) * kvw] = og


def _swa_prompt_attn(q, kt, vt, rel_bias, sinks):
    b, s, d = q.shape
    kvw = kt.shape[1]
    nb = s // WINDOW
    dist = np.arange(WINDOW)[:, None] - np.arange(2 * WINDOW)[None, :] + WINDOW
    bucket = jnp.asarray(_t5_bucket_np(dist))
    prev = pl.BlockSpec((1, kvw, WINDOW), lambda bi, n: (bi, 0, jnp.maximum(n - 1, 0)))
    cur = pl.BlockSpec((1, kvw, WINDOW), lambda bi, n: (bi, 0, n))
    smem = pl.BlockSpec(memory_space=pltpu.SMEM)
    return pl.pallas_call(
        functools.partial(_swa_prompt_kernel, scale=SWA_HEAD_DIM ** -0.5),
        grid=(b, nb),
        in_specs=[smem, smem,
                  pl.BlockSpec((1, WINDOW, d), lambda bi, n: (bi, n, 0)),
                  prev, cur, prev, cur,
                  pl.BlockSpec((WINDOW, 2 * WINDOW), lambda bi, n: (0, 0))],
        out_specs=pl.BlockSpec((1, WINDOW, d), lambda bi, n: (bi, n, 0)),
        out_shape=jax.ShapeDtypeStruct((b, s, d), F32),
        scratch_shapes=[pltpu.VMEM((SWA_HEADS, WINDOW, 2 * WINDOW), F32)],
        compiler_params=_params("arbitrary", "arbitrary"),
        name="swa_prompt_attn",
    )(rel_bias.reshape(-1), sinks, q, kt, kt, vt, vt, bucket)


def _swa_sample_kernel(q_ref, ck_ref, cv_ref, kn_ref, vn_ref, sink_ref, relrow_ref, bucket_ref, o_ref,
                       bias_ref, kpad_ref, vpad_ref, *, t_new, scale):
    per_kv = t_new * SWA_GROUP
    rows = SWA_KV_HEADS * per_kv
    kvw = SWA_KV_HEADS * SWA_HEAD_DIM

    @pl.when(pl.program_id(0) == 0)
    def _():
        bucket = bucket_ref[...]
        acc = jnp.zeros(bucket.shape, F32)
        for bk in range(REL_BUCKETS):
            acc = jnp.where(bucket == bk, relrow_ref[:, bk:bk + 1], acc)
        bias_ref[...] = acc
        kpad_ref[...] = jnp.zeros_like(kpad_ref)
        vpad_ref[...] = jnp.zeros_like(vpad_ref)

    kpad_ref[0:SUBLANES, :] = kn_ref[0]
    vpad_ref[0:SUBLANES, :] = vn_ref[0]
    q16 = q_ref[0]
    lane_q = lax.broadcasted_iota(jnp.int32, (per_kv, kvw), 1)
    qexp = jnp.concatenate([jnp.where((lane_q // SWA_HEAD_DIM) == k, q16, 0.0) for k in range(SWA_KV_HEADS)],
                           axis=0).astype(BF16)
    ckt = ck_ref[0].reshape(kvw, WINDOW).astype(BF16)
    cvt = cv_ref[0].reshape(kvw, WINDOW).astype(BF16)
    zc = _dot(qexp, ckt) * scale + bias_ref[:, :WINDOW]
    zn = _dot_nt(qexp, kpad_ref[...].astype(BF16)) * scale + bias_ref[:, WINDOW:]
    row = lax.broadcasted_iota(jnp.int32, (rows, WINDOW), 0)
    key = lax.broadcasted_iota(jnp.int32, (rows, WINDOW), 1)
    t_row = (row % per_kv) // SWA_GROUP
    zc = jnp.where(key >= t_row, zc, -jnp.inf)
    zn = jnp.where(key <= t_row, zn, -jnp.inf)
    sink = sink_ref[:, 0:1]
    m = jnp.maximum(jnp.maximum(jnp.max(zc, axis=-1, keepdims=True), jnp.max(zn, axis=-1, keepdims=True)), sink)
    pc = jnp.exp(zc - m)
    pn = jnp.exp(zn - m)
    den = jnp.sum(pc, axis=-1, keepdims=True) + jnp.sum(pn, axis=-1, keepdims=True) + jnp.exp(sink - m)
    o = _dot_nt((pc / den).astype(BF16), cvt) + _dot((pn / den).astype(BF16), vpad_ref[...].astype(BF16))
    lane_o = lax.broadcasted_iota(jnp.int32, (per_kv, kvw), 1)
    res = jnp.zeros((per_kv, kvw), F32)
    for k in range(SWA_KV_HEADS):
        res = jnp.where((lane_o // SWA_HEAD_DIM) == k, o[k * per_kv:(k + 1) * per_kv, :], res)
    o_ref[0] = res


def _swa_sample_attn(q, k_new, v_new, cache_kt, cache_vt, rel_bias, sinks):
    n, t_new, d = q.shape
    kvw = k_new.shape[-1]
    per_kv = t_new * SWA_GROUP
    rows = SWA_KV_HEADS * per_kv
    pad = ((0, 0), (0, SUBLANES - t_new), (0, 0))
    r = np.arange(rows)
    t_of_row = (r % per_kv) // SWA_GROUP
    head_of_row = (r // per_kv) * SWA_GROUP + (r % SWA_GROUP)
    c = np.arange(2 * WINDOW)
    dist = np.where(c[None, :] < WINDOW, t_of_row[:, None] + WINDOW - c[None, :],
                    t_of_row[:, None] - (c[None, :] - WINDOW))
    bucket = jnp.asarray(_t5_bucket_np(dist))
    relrow = rel_bias.T[head_of_row]
    sink_tile = jnp.broadcast_to(sinks[head_of_row][:, None], (rows, LANES))
    per_seq = lambda s: (s, 0, 0)
    per_seq4 = lambda s: (s, 0, 0, 0)
    const2 = lambda s: (0, 0)
    cache_spec = pl.BlockSpec((1, SWA_KV_HEADS, SWA_HEAD_DIM, WINDOW), per_seq4)
    out = pl.pallas_call(
        functools.partial(_swa_sample_kernel, t_new=t_new, scale=SWA_HEAD_DIM ** -0.5),
        grid=(n,),
        in_specs=[pl.BlockSpec((1, per_kv, kvw), per_seq),
                  cache_spec, cache_spec,
                  pl.BlockSpec((1, SUBLANES, kvw), per_seq),
                  pl.BlockSpec((1, SUBLANES, kvw), per_seq),
                  pl.BlockSpec((rows, LANES), const2),
                  pl.BlockSpec((rows, REL_BUCKETS), const2),
                  pl.BlockSpec((rows, 2 * WINDOW), const2)],
        out_specs=pl.BlockSpec((1, per_kv, kvw), per_seq),
        out_shape=jax.ShapeDtypeStruct((n, per_kv, kvw), F32),
        scratch_shapes=[pltpu.VMEM((rows, 2 * WINDOW), F32),
                        pltpu.VMEM((WINDOW, kvw), F32),
                        pltpu.VMEM((WINDOW, kvw), F32)],
        compiler_params=_params("arbitrary"),
        name="swa_sample_attn",
    )(q.reshape(n, per_kv, kvw), cache_kt, cache_vt, jnp.pad(k_new, pad), jnp.pad(v_new, pad),
      sink_tile, relrow, bucket)
    return out.reshape(n, t_new, d)


def _conv_post(acc, gate, lng, lnb):
    if gate is not None:
        return gate * acc
    y = _layer_norm(acc, lng, lnb)
    return y * jax.nn.sigmoid(y)


def _conv_prompt_kernel(u_ref, halo_ref, *rest, width, halo, gated):
    if gated:
        gate_ref, w_ref, b_ref, y_ref, ext_ref = rest
    else:
        w_ref, b_ref, lng_ref, lnb_ref, y_ref, ext_ref = rest
    ts = u_ref.shape[1]
    c = u_ref.shape[2]
    s = pl.program_id(1)
    ext_ref[0:halo, :] = jnp.where(s > 0, halo_ref[0], 0.0)
    ext_ref[halo:halo + ts, :] = u_ref[0]
    base = halo - (width - 1)
    for r0 in range(0, ts, CONV_ROWS):
        acc = jnp.broadcast_to(b_ref[...], (CONV_ROWS, c))
        for k in range(width):
            acc = acc + w_ref[k:k + 1, :] * ext_ref[r0 + base + k:r0 + base + k + CONV_ROWS, :]
        if gated:
            y = _conv_post(acc, gate_ref[0, r0:r0 + CONV_ROWS, :], None, None)
        else:
            y = _conv_post(acc, None, lng_ref[...], lnb_ref[...])
        y_ref[0, r0:r0 + CONV_ROWS, :] = y


def _conv_prompt(u, w, b, gate=None, lng=None, lnb=None):
    bsz, s, c = u.shape
    width = w.shape[0]
    halo = SUBLANES * (-(-(width - 1) // SUBLANES))
    ts = min(CONV_TILE, s)
    per_tile = ts // halo
    tile = pl.BlockSpec((1, ts, c), lambda bi, si: (bi, si, 0))
    vec = pl.BlockSpec((1, c), lambda bi, si: (0, 0))
    in_specs = [tile, pl.BlockSpec((1, halo, c), lambda bi, si: (bi, jnp.maximum(si * per_tile - 1, 0), 0))]
    args = [u, u]
    if gate is not None:
        in_specs += [tile]
        args += [gate]
    in_specs += [pl.BlockSpec((width, c), lambda bi, si: (0, 0)), vec]
    args += [w, b.reshape(1, c)]
    if gate is None:
        in_specs += [vec, vec]
        args += [lng.reshape(1, c), lnb.reshape(1, c)]
    return pl.pallas_call(
        functools.partial(_conv_prompt_kernel, width=width, halo=halo, gated=gate is not None),
        grid=(bsz, s // ts),
        in_specs=in_specs,
        out_specs=tile,
        out_shape=jax.ShapeDtypeStruct((bsz, s, c), F32),
        scratch_shapes=[pltpu.VMEM((halo + ts, c), F32)],
        compiler_params=_params("arbitrary", "arbitrary"),
        name="conv_prompt_w%d" % width,
    )(*args)


def _conv_sample_kernel(full_ref, *rest, width, gated):
    if gated:
        gate_ref, w_ref, b_ref, y_ref = rest
    else:
        w_ref, b_ref, lng_ref, lnb_ref, y_ref = rest
    t_new, nseq, c = y_ref.shape
    for t in range(t_new):
        acc = jnp.broadcast_to(b_ref[...], (nseq, c))
        for k in range(width):
            acc = acc + w_ref[k:k + 1, :] * full_ref[t + k]
        if gated:
            y = _conv_post(acc, gate_ref[t], None, None)
        else:
            y = _conv_post(acc, None, lng_ref[...], lnb_ref[...])
        y_ref[t] = y


def _conv_sample(full, w, b, gate=None, lng=None, lnb=None):
    rows, n, c = full.shape
    width = w.shape[0]
    t_new = rows - (width - 1)
    ns = min(CONV_SEQS, n)
    seqs = lambda r: pl.BlockSpec((r, ns, c), lambda i: (0, i, 0))
    vec = pl.BlockSpec((1, c), lambda i: (0, 0))
    in_specs = [seqs(rows)]
    args = [full]
    if gate is not None:
        in_specs += [seqs(t_new)]
        args += [gate]
    in_specs += [pl.BlockSpec((width, c), lambda i: (0, 0)), vec]
    args += [w, b.reshape(1, c)]
    if gate is None:
        in_specs += [vec, vec]
        args += [lng.reshape(1, c), lnb.reshape(1, c)]
    return pl.pallas_call(
        functools.partial(_conv_sample_kernel, width=width, gated=gate is not None),
        grid=(n // ns,),
        in_specs=in_specs,
        out_specs=seqs(t_new),
        out_shape=jax.ShapeDtypeStruct((t_new, n, c), F32),
        compiler_params=_params("arbitrary"),
        name="conv_sample_w%d" % width,
    )(*args)


def _cols(w, n):
    return [c.astype(BF16) for c in jnp.split(w, n, axis=1)]


def _time_major(x):
    return jnp.transpose(x, (1, 0, 2))


def _heads_last(xt, heads, head_dim):
    g, _, s = xt.shape
    return jnp.transpose(xt.reshape(g, heads, head_dim, s), (0, 3, 1, 2))


def _positions_last(x):
    return jnp.transpose(x, (0, 2, 3, 1))


def kernel(x_prompt, x_sample, cache_sb_k, cache_sb_v, page_table, state_gc_conv, cache_swa_k, cache_swa_v, state_cf_conv, sb_w_qkv, sb_bias, sb_w_o, gc_w_in, gc_conv_w, gc_conv_b, gc_w_out, swa_w_qkv, swa_sinks, swa_w_o, rel_bias, cf_w_in, cf_conv_w, cf_conv_b, cf_ln_g, cf_ln_b, cf_w_out, ffn_w_gate, ffn_w_up, ffn_w_down, ln_g, ln_b):
    bsz, seq, d = x_prompt.shape
    n, t_new, _ = x_sample.shape
    m_p, m_s = bsz * seq, n * t_new
    xp = x_prompt.reshape(m_p, d)
    xs = x_sample.reshape(m_s, d)

    def ln_params(i, j):
        return ln_g[i, j].reshape(1, d), ln_b[i, j].reshape(1, d)

    def ffn(x, i):
        g, b = ln_params(i, 1)
        return _ffn_ln(x, ffn_w_gate[i].astype(BF16), ffn_w_up[i].astype(BF16), ffn_w_down[i].astype(BF16), g, b)

    def as_p(x):
        return x.reshape(bsz, seq, d)

    def as_s(x):
        return x.reshape(1, m_s, d)

    g0, b0 = ln_params(0, 0)
    w_q, w_k, w_v = _cols(sb_w_qkv, 3)
    w_sb_o = sb_w_o.astype(BF16)
    qp, kt_p, vt_p, kt_b, vt_b = _mm(
        as_p(xp), [w_q], [w_k.T, w_v.T], lambda ps, ts: (ps[0] * SB_SCALE, ts[0], ts[1], ts[0], ts[1]),
        [('n', BF16), ('t', F32), ('t', F32), ('t', BF16), ('t', BF16)])
    o = _sb_prompt_attn(qp, kt_b, vt_b, sb_bias)
    xp = _proj_ln(o.reshape(m_p, d), w_sb_o, xp, g0, b0)
    sb_k_p = _heads_last(kt_p, SB_HEADS, SB_HEAD_DIM)
    sb_v_p = _heads_last(vt_p, SB_HEADS, SB_HEAD_DIM)
    qs, k_s, v_s = _mm(as_s(xs), [w_q, w_k, w_v], [], lambda ps, ts: ps, [('n', F32)] * 3)
    qs, k_s, v_s = (a.reshape(n, t_new, d) for a in (qs, k_s, v_s))
    o = _sb_sample_attn(qs, k_s, v_s, _positions_last(cache_sb_k), _positions_last(cache_sb_v), page_table, sb_bias)
    xs = _proj_ln(o.reshape(m_s, d), w_sb_o, xs, g0, b0)
    sb_k_s = k_s.reshape(n, t_new, SB_HEADS, SB_HEAD_DIM)
    sb_v_s = v_s.reshape(n, t_new, SB_HEADS, SB_HEAD_DIM)
    xp, xs = ffn(xp, 0), ffn(xs, 0)

    g1, b1 = ln_params(1, 0)
    w_gc = _cols(gc_w_in, 3)
    w_gc_o = gc_w_out.astype(BF16)
    gc_epilogue = lambda ps, ts: (ps[0], ps[1] * ps[2])
    gc_outs = [('n', F32), ('n', F32)]
    gate_p, u_p = _mm(as_p(xp), w_gc, [], gc_epilogue, gc_outs)
    y = _conv_prompt(u_p, gc_conv_w, gc_conv_b, gate=gate_p)
    xp = _proj_ln(y.reshape(m_p, d), w_gc_o, xp, g1, b1)
    gc_p = u_p[:, seq - (GC_WIDTH - 1):]
    gate_s, u_s = _mm(as_s(xs), w_gc, [], gc_epilogue, gc_outs)
    full = jnp.concatenate([_time_major(state_gc_conv), _time_major(u_s.reshape(n, t_new, d))], axis=0)
    y = _conv_sample(full, gc_conv_w, gc_conv_b, gate=_time_major(gate_s.reshape(n, t_new, d)))
    xs = _proj_ln(_time_major(y).reshape(m_s, d), w_gc_o, xs, g1, b1)
    gc_s = _time_major(full[full.shape[0] - (GC_WIDTH - 1):])
    xp, xs = ffn(xp, 1), ffn(xs, 1)

    g2, b2 = ln_params(2, 0)
    qd = SWA_HEADS * SWA_HEAD_DIM
    kvw = SWA_KV_HEADS * SWA_HEAD_DIM
    w_q = swa_w_qkv[:, :qd].reshape(d, SWA_KV_HEADS, SWA_GROUP, SWA_HEAD_DIM)
    w_q = w_q.transpose(0, 2, 1, 3).reshape(d, qd).astype(BF16)
    w_k, w_v = _cols(swa_w_qkv[:, qd:], 2)
    w_swa_o = swa_w_o.reshape(SWA_KV_HEADS, SWA_GROUP, SWA_HEAD_DIM, d).transpose(1, 0, 2, 3)
    w_swa_o = w_swa_o.reshape(qd, d).astype(BF16)
    (qp,) = _mm(as_p(xp), [w_q], [], lambda ps, ts: ps, [('n', BF16)])
    kt_p, vt_p, kt_b, vt_b = _mm(
        as_p(xp), [], [w_k.T, w_v.T], lambda ps, ts: (ts[0], ts[1], ts[0], ts[1]),
        [('t', F32), ('t', F32), ('t', BF16), ('t', BF16)], tn=kvw)
    o = _swa_prompt_attn(qp, kt_b, vt_b, rel_bias, swa_sinks)
    xp = _proj_ln(o.reshape(m_p, qd), w_swa_o, xp, g2, b2)
    buf = min(WINDOW, seq)
    swa_k_p = _heads_last(kt_p[:, :, seq - buf:], SWA_KV_HEADS, SWA_HEAD_DIM)
    swa_v_p = _heads_last(vt_p[:, :, seq - buf:], SWA_KV_HEADS, SWA_HEAD_DIM)
    (qs,) = _mm(as_s(xs), [w_q], [], lambda ps, ts: ps, [('n', F32)])
    k_s, v_s = _mm(as_s(xs), [w_k, w_v], [], lambda ps, ts: ps, [('n', F32)] * 2, tn=kvw)
    k_s, v_s = k_s.reshape(n, t_new, kvw), v_s.reshape(n, t_new, kvw)
    o = _swa_sample_attn(qs.reshape(n, t_new, qd), k_s, v_s, _positions_last(cache_swa_k),
                         _positions_last(cache_swa_v), rel_bias, swa_sinks)
    xs = _proj_ln(o.reshape(m_s, qd), w_swa_o, xs, g2, b2)
    kv_heads = (SWA_KV_HEADS, SWA_HEAD_DIM)
    swa_k_s = jnp.concatenate([cache_swa_k, k_s.reshape(n, t_new, *kv_heads)], axis=1)[:, t_new:]
    swa_v_s = jnp.concatenate([cache_swa_v, v_s.reshape(n, t_new, *kv_heads)], axis=1)[:, t_new:]
    xp, xs = ffn(xp, 2), ffn(xs, 2)

    g3, b3 = ln_params(3, 0)
    w_cf = _cols(cf_w_in, 2)
    w_cf_o = cf_w_out.astype(BF16)
    cf_epilogue = lambda ps, ts: (ps[0] * jax.nn.sigmoid(ps[1]),)
    (u_p,) = _mm(as_p(xp), w_cf, [], cf_epilogue, [('n', F32)])
    y = _conv_prompt(u_p, cf_conv_w, cf_conv_b, lng=cf_ln_g, lnb=cf_ln_b)
    xp = _proj_ln(y.reshape(m_p, d), w_cf_o, xp, g3, b3)
    cf_p = u_p[:, seq - (CF_WIDTH - 1):]
    (u_s,) = _mm(as_s(xs), w_cf, [], cf_epilogue, [('n', F32)])
    full = jnp.concatenate([_time_major(state_cf_conv), _time_major(u_s.reshape(n, t_new, d))], axis=0)
    y = _conv_sample(full, cf_conv_w, cf_conv_b, lng=cf_ln_g, lnb=cf_ln_b)
    xs = _proj_ln(_time_major(y).reshape(m_s, d), w_cf_o, xs, g3, b3)
    cf_s = _time_major(full[full.shape[0] - (CF_WIDTH - 1):])
    xp, xs = ffn(xp, 3), ffn(xs, 3)

    return (xp.reshape(bsz, seq, d), xs.reshape(n, t_new, d), sb_k_p, sb_v_p, sb_k_s, sb_v_s,
            gc_p, gc_s, swa_k_p, swa_v_p, swa_k_s, swa_v_s, cf_p, cf_s)
```
